```python
import jax, jax.numpy as jnp
from jax import lax
import numpy as np

D_MODEL = 4096
BATCH = 1
SEQ = 8192
DEPTH = 4

CHUNK = 64
N_A_LAYERS = DEPTH // 2
N_B_LAYERS = DEPTH - N_A_LAYERS
CONV_KERNEL = 31
FFN_CONV_KERNEL = 3
D_FF = 10944
N_HEADS = 32
HEAD_DIM = D_MODEL // N_HEADS
Q_BLOCK = 128
COND_RANK = 512
N_MOD = 6
EPS = 1e-6

kernel_name = 'yoco_conformer_fox_hybrid'


def _normal(key, shape, scale):
    return jax.random.normal(key, shape, jnp.float32) * scale


def setup_inputs(seed: int = 0) -> dict:
    key = jax.random.key(seed)
    ks = jax.random.split(key, 32)
    D, F, H, R, K = D_MODEL, D_FF, N_HEADS, COND_RANK, CONV_KERNEL
    NA, NB = N_A_LAYERS, N_B_LAYERS
    return {
        'x': _normal(ks[0], (BATCH, SEQ, D), 1.0),
        'c': _normal(ks[1], (BATCH, D), 1.0),
        'w_cond': _normal(ks[2], (D, R), D ** -0.5),
        'b_cond': _normal(ks[3], (R,), 0.02),
        'w_ada': _normal(ks[4], (DEPTH, R, N_MOD * D), 0.5 * R ** -0.5),
        'b_ada': _normal(ks[5], (DEPTH, N_MOD * D), 0.02),
        'pre_mix_g': 1.0 + _normal(ks[6], (DEPTH, D), 0.02),
        'post_mix_g': 1.0 + _normal(ks[7], (DEPTH, D), 0.02),
        'pre_ffn_g': 1.0 + _normal(ks[8], (DEPTH, D), 0.02),
        'post_ffn_g': 1.0 + _normal(ks[9], (DEPTH, D), 0.02),
        'conf_w1': _normal(ks[10], (NA, D, 2 * D), D ** -0.5),
        'conf_b1': _normal(ks[11], (NA, 2 * D), 0.02),
        'conf_dw': _normal(ks[12], (NA, K, D), K ** -0.5),
        'conf_dw_b': _normal(ks[13], (NA, D), 0.02),
        'conf_ln_g': 1.0 + _normal(ks[14], (NA, D), 0.02),
        'conf_ln_b': _normal(ks[15], (NA, D), 0.02),
        'conf_w2': _normal(ks[16], (NA, D, D), D ** -0.5),
        'conf_b2': _normal(ks[17], (NA, D), 0.02),
        'w_ada_kv': _normal(ks[18], (R, 2 * D), 0.5 * R ** -0.5),
        'b_ada_kv': _normal(ks[19], (2 * D,), 0.02),
        'kv_norm_g': 1.0 + _normal(ks[20], (D,), 0.02),
        'w_kv': _normal(ks[21], (D, 2 * D), D ** -0.5),
        'w_f': _normal(ks[22], (D, H), D ** -0.5),
        'b_f': _normal(ks[23], (H,), 0.02),
        'w_q': _normal(ks[24], (NB, D, D), D ** -0.5),
        'w_o': _normal(ks[25], (NB, D, D), D ** -0.5),
        'w_up': _normal(ks[26], (DEPTH, D, 2 * F), D ** -0.5),
        'ffn_dw': _normal(ks[27], (DEPTH, FFN_CONV_KERNEL, 2 * F), FFN_CONV_KERNEL ** -0.5),
        'w_down': _normal(ks[28], (DEPTH, F, D), F ** -0.5),
    }


def rms_norm(x, g):
    xf = x.astype(jnp.float32)
    y = xf * lax.rsqrt(jnp.mean(xf * xf, axis=-1, keepdims=True) + EPS)
    return (y * g.astype(jnp.float32)).astype(x.dtype)


def layer_norm(x, g, b):
    xf = x.astype(jnp.float32)
    mu = jnp.mean(xf, axis=-1, keepdims=True)
    var = jnp.mean(jnp.square(xf - mu), axis=-1, keepdims=True)
    y = (xf - mu) * lax.rsqrt(var + EPS)
    return (y * g.astype(jnp.float32) + b.astype(jnp.float32)).astype(x.dtype)


def modulate(h, shift, scale):
    return h * (1 + scale[:, None, :]) + shift[:, None, :]


def causal_depthwise_conv(x, w):
    kw = w.shape[0]
    s = x.shape[1]
    xp = jnp.pad(x, ((0, 0), (kw - 1, 0), (0, 0)))
    y = xp[:, 0:s, :] * w[0]
    for k in range(1, kw):
        y = y + xp[:, k:k + s, :] * w[k]
    return y


def conformer_conv(h, w1, b1, dw, dw_b, ln_g, ln_b, w2, b2):
    u = jax.nn.glu(h @ w1 + b1, axis=-1)
    u = causal_depthwise_conv(u, dw) + dw_b
    u = jax.nn.silu(layer_norm(u, ln_g, ln_b))
    return u @ w2 + b2


def conv_ffn(h, w_up, dw, w_down):
    u = causal_depthwise_conv(h @ w_up, dw)
    gate, val = jnp.split(u, 2, axis=-1)
    return (jax.nn.gelu(gate, approximate=True) * val) @ w_down


def shared_kv(x, cond, w_ada_kv, b_ada_kv, kv_norm_g, w_kv, w_f, b_f):
    bsz, s, _ = x.shape
    sh, sc = jnp.split(cond @ w_ada_kv + b_ada_kv, 2, axis=-1)
    h = modulate(rms_norm(x, kv_norm_g), sh, sc)
    k, v = jnp.split(h @ w_kv, 2, axis=-1)
    k = k.reshape(bsz, s, N_HEADS, HEAD_DIM)
    v = v.reshape(bsz, s, N_HEADS, HEAD_DIM)
    log_f = jax.nn.log_sigmoid((h @ w_f + b_f).astype(jnp.float32))
    cum = jnp.cumsum(log_f, axis=1)
    return k, v, cum


def fox_attention(q, k, v, cum):
    bsz, s, h, dh = q.shape
    nb = s // Q_BLOCK
    scale = dh ** -0.5
    q_blocks = q.reshape(bsz, nb, Q_BLOCK, h, dh).transpose(1, 0, 2, 3, 4)
    cq_blocks = cum.reshape(bsz, nb, Q_BLOCK, h).transpose(1, 0, 3, 2)
    ck = cum.transpose(0, 2, 1)
    kpos = jnp.arange(s)

    def one_block(args):
        qi, cqi, bi = args
        qpos = bi * Q_BLOCK + jnp.arange(Q_BLOCK)
        logits = jnp.einsum('bqhd,bkhd->bhqk', qi, k).astype(jnp.float32) * scale
        logits = logits + (cqi[:, :, :, None] - ck[:, :, None, :])
        logits = jnp.where(kpos[None, :] <= qpos[:, None], logits, -jnp.inf)
        p = jax.nn.softmax(logits, axis=-1).astype(v.dtype)
        return jnp.einsum('bhqk,bkhd->bqhd', p, v)

    out = lax.map(one_block, (q_blocks, cq_blocks, jnp.arange(nb)))
    return out.transpose(1, 0, 2, 3, 4).reshape(bsz, s, h, dh)


def reference(x, c, w_cond, b_cond, w_ada, b_ada, pre_mix_g, post_mix_g, pre_ffn_g,
              post_ffn_g, conf_w1, conf_b1, conf_dw, conf_dw_b, conf_ln_g, conf_ln_b,
              conf_w2, conf_b2, w_ada_kv, b_ada_kv, kv_norm_g, w_kv, w_f, b_f, w_q,
              w_o, w_up, ffn_dw, w_down):
    bsz, s, d = x.shape
    cond = jax.nn.silu(c @ w_cond + b_cond)
    k = v = cum = None
    for l in range(DEPTH):
        mod = cond @ w_ada[l] + b_ada[l]
        sh_m, sc_m, g_m, sh_f, sc_f, g_f = jnp.split(mod, N_MOD, axis=-1)
        h = modulate(rms_norm(x, pre_mix_g[l]), sh_m, sc_m)
        if l < N_A_LAYERS:
            a = l
            y = conformer_conv(h, conf_w1[a], conf_b1[a], conf_dw[a], conf_dw_b[a],
                               conf_ln_g[a], conf_ln_b[a], conf_w2[a], conf_b2[a])
        else:
            if l == N_A_LAYERS:
                k, v, cum = shared_kv(x, cond, w_ada_kv, b_ada_kv, kv_norm_g,
                                      w_kv, w_f, b_f)
            b = l - N_A_LAYERS
            q = (h @ w_q[b]).reshape(bsz, s, N_HEADS, HEAD_DIM)
            y = fox_attention(q, k, v, cum).reshape(bsz, s, d) @ w_o[b]
        x = x + g_m[:, None, :] * rms_norm(y, post_mix_g[l])
        h = modulate(rms_norm(x, pre_ffn_g[l]), sh_f, sc_f)
        y = conv_ffn(h, w_up[l], ffn_dw[l], w_down[l])
        x = x + g_f[:, None, :] * rms_norm(y, post_ffn_g[l])
    return x
```

```python
import functools

import jax
import jax.numpy as jnp
from jax import lax
from jax.experimental import pallas as pl
from jax.experimental.pallas import tpu as pltpu

EPS = 1e-6
HEAD_DIM = 128
LANES = 128
SUBLANES = 8
HALO_ROWS = 32
VMEM_LIMIT_BYTES = 56 * 1024 * 1024

F32 = jnp.float32
BF16 = jnp.bfloat16


def _round_up(n, m):
    return (n + m - 1) // m * m


def _tile(n, want):
    if n <= want:
        return n
    t = want
    while n % t:
        t //= 2
    return t


def _params(sem):
    return pltpu.CompilerParams(dimension_semantics=sem, vmem_limit_bytes=VMEM_LIMIT_BYTES)


def _cond_kernel(c_ref, w_ref, b_ref, o_ref):
    z = jnp.sum(w_ref[...] * c_ref[...], axis=0, keepdims=True) + b_ref[...]
    o_ref[...] = z * jax.nn.sigmoid(z)


def _cond(c_col, w_cond, b_cond):
    d, r = w_cond.shape
    return pl.pallas_call(
        _cond_kernel,
        out_shape=jax.ShapeDtypeStruct((1, r), F32),
        compiler_params=_params(None),
        name="cond",
    )(c_col, w_cond, b_cond.reshape(1, r))


def _mod_kernel(cc_ref, w_ref, b_ref, o_ref):
    o_ref[...] = jnp.sum(w_ref[...] * cc_ref[...], axis=0, keepdims=True) + b_ref[...]


def _mod(cond_col, w, b):
    nl, r, n = w.shape
    tn = _tile(n, 2048)
    return pl.pallas_call(
        _mod_kernel,
        grid=(nl, n // tn),
        in_specs=[
            pl.BlockSpec((r, 1), lambda l, j: (0, 0)),
            pl.BlockSpec((None, r, tn), lambda l, j: (l, 0, j)),
            pl.BlockSpec((None, 1, tn), lambda l, j: (l, 0, j)),
        ],
        out_specs=pl.BlockSpec((None, 1, tn), lambda l, j: (l, 0, j)),
        out_shape=jax.ShapeDtypeStruct((nl, 1, n), F32),
        compiler_params=_params(("parallel", "parallel")),
        name="mod",
    )(cond_col, w, b.reshape(nl, 1, n))


def _unit_rms(v):
    return v * lax.rsqrt(jnp.mean(v * v, axis=-1, keepdims=True) + EPS)


def _prenorm_kernel(x_ref, g_ref, sc_ref, sh_ref, h_ref):
    xn = _unit_rms(x_ref[...]) * g_ref[...]
    h_ref[...] = (xn * (1.0 + sc_ref[...]) + sh_ref[...]).astype(h_ref.dtype)


def _prenorm(x, g, sc, sh):
    s, d = x.shape
    tm = _tile(s, 256)
    row = pl.BlockSpec((1, d), lambda i: (0, 0))
    return pl.pallas_call(
        _prenorm_kernel,
        grid=(s // tm,),
        in_specs=[pl.BlockSpec((tm, d), lambda i: (i, 0)), row, row, row],
        out_specs=pl.BlockSpec((tm, d), lambda i: (i, 0)),
        out_shape=jax.ShapeDtypeStruct((s, d), BF16),
        compiler_params=_params(("parallel",)),
        name="prenorm",
    )(x, g, sc, sh)


def _post_kernel(*refs, n_norm):
    y_ref, x_ref, pg_ref, gate_ref = refs[:4]
    norm_refs = refs[4:4 + 3 * n_norm]
    xo_ref = refs[4 + 3 * n_norm]
    h_refs = refs[5 + 3 * n_norm:]
    xn = x_ref[...] + gate_ref[...] * (_unit_rms(y_ref[...]) * pg_ref[...])
    xo_ref[...] = xn
    if n_norm:
        r = _unit_rms(xn)
        for t in range(n_norm):
            g_ref, sc_ref, sh_ref = norm_refs[3 * t:3 * t + 3]
            h_refs[t][...] = ((r * g_ref[...]) * (1.0 + sc_ref[...]) + sh_ref[...]).astype(BF16)


def _post(y, x, post_g, gate, norms):
    s, d = x.shape
    tm = _tile(s, 256)
    blk = pl.BlockSpec((tm, d), lambda i: (i, 0))
    row = pl.BlockSpec((1, d), lambda i: (0, 0))
    n_norm = len(norms)
    flat = [v for trip in norms for v in trip]
    outs = pl.pallas_call(
        functools.partial(_post_kernel, n_norm=n_norm),
        grid=(s // tm,),
        in_specs=[blk, blk, row, row] + [row] * (3 * n_norm),
        out_specs=[blk] * (1 + n_norm),
        out_shape=[jax.ShapeDtypeStruct((s, d), F32)] + [jax.ShapeDtypeStruct((s, d), BF16)] * n_norm,
        compiler_params=_params(("parallel",)),
        name="post",
    )(y, x, post_g, gate, *flat)
    return outs[0], list(outs[1:])


def _mm_kernel(*refs, has_bias, scale):
    if has_bias:
        a_ref, w_ref, b_ref, o_ref = refs
    else:
        a_ref, w_ref, o_ref = refs
    acc = jnp.dot(a_ref[...], w_ref[...], preferred_element_type=F32)
    if has_bias:
        acc = acc + b_ref[...]
    if scale is not None:
        acc = acc * scale
    o_ref[...] = acc.astype(o_ref.dtype)


def _matmul(a, w, bias=None, scale=None, out_dtype=F32, tm=1024, tn=1024):
    m, k = a.shape
    n = w.shape[1]
    tm, tn = _tile(m, tm), _tile(n, tn)
    in_specs = [pl.BlockSpec((tm, k), lambda i, j: (i, 0)), pl.BlockSpec((k, tn), lambda i, j: (0, j))]
    args = [a, w]
    if bias is not None:
        in_specs.append(pl.BlockSpec((1, tn), lambda i, j: (0, j)))
        args.append(bias)
    return pl.pallas_call(
        functools.partial(_mm_kernel, has_bias=bias is not None, scale=scale),
        grid=(m // tm, n // tn),
        in_specs=in_specs,
        out_specs=pl.BlockSpec((tm, tn), lambda i, j: (i, j)),
        out_shape=jax.ShapeDtypeStruct((m, n), out_dtype),
        compiler_params=_params(("parallel", "parallel")),
        name="matmul",
    )(*args)


def _glu_kernel(a_ref, wa_ref, wg_ref, ba_ref, bg_ref, o_ref):
    a = a_ref[...]
    lin = jnp.dot(a, wa_ref[...], preferred_element_type=F32) + ba_ref[...]
    gate = jnp.dot(a, wg_ref[...], preferred_element_type=F32) + bg_ref[...]
    o_ref[...] = lin * jax.nn.sigmoid(gate)


def _glu_matmul(a, w1, b1, tm=1024, tn=512):
    m, k = a.shape
    d = w1.shape[1] // 2
    tm, tn = _tile(m, tm), _tile(d, tn)
    nj = d // tn
    return pl.pallas_call(
        _glu_kernel,
        grid=(m // tm, nj),
        in_specs=[
            pl.BlockSpec((tm, k), lambda i, j: (i, 0)),
            pl.BlockSpec((k, tn), lambda i, j: (0, j)),
            pl.BlockSpec((k, tn), lambda i, j: (0, j + nj)),
            pl.BlockSpec((1, tn), lambda i, j: (0, j)),
            pl.BlockSpec((1, tn), lambda i, j: (0, j + nj)),
        ],
        out_specs=pl.BlockSpec((tm, tn), lambda i, j: (i, j)),
        out_shape=jax.ShapeDtypeStruct((m, d), F32),
        compiler_params=_params(("parallel", "parallel")),
        name="glu_matmul",
    )(a, w1, w1, b1, b1)


def _conf_conv_kernel(u_ref, halo_ref, dw_ref, dwb_ref, lng_ref, lnb_ref, o_ref, src_ref, acc_ref,
                      *, kw, row_chunk, col_chunk):
    tm, d = u_ref.shape
    first = pl.program_id(0) == 0
    halo = halo_ref[...]
    src_ref[0:HALO_ROWS, :] = jnp.where(first, jnp.zeros_like(halo), halo)
    src_ref[HALO_ROWS:HALO_ROWS + tm, :] = u_ref[...]
    lead = HALO_ROWS - (kw - 1)

    def conv_cols(cc, carry):
        cols = pl.ds(pl.multiple_of(cc * col_chunk, col_chunk), col_chunk)
        for r0 in range(0, tm, row_chunk):
            acc = src_ref[r0 + lead:r0 + lead + row_chunk, cols] * dw_ref[0:1, cols]
            for k in range(1, kw):
                acc = acc + src_ref[r0 + lead + k:r0 + lead + k + row_chunk, cols] * dw_ref[k:k + 1, cols]
            acc_ref[r0:r0 + row_chunk, cols] = acc + dwb_ref[:, cols]
        return carry

    lax.fori_loop(0, d // col_chunk, conv_cols, 0)

    def norm_rows(rc, carry):
        r0 = pl.multiple_of(rc * row_chunk, row_chunk)
        v = acc_ref[pl.ds(r0, row_chunk), :]
        mu = jnp.mean(v, axis=-1, keepdims=True)
        cen = v - mu
        var = jnp.mean(cen * cen, axis=-1, keepdims=True)
        y = cen * lax.rsqrt(var + EPS) * lng_ref[...] + lnb_ref[...]
        o_ref[pl.ds(r0, row_chunk), :] = (y * jax.nn.sigmoid(y)).astype(o_ref.dtype)
        return carry

    lax.fori_loop(0, tm // row_chunk, norm_rows, 0)


def _conf_conv(u, dw, dw_b, ln_g, ln_b):
    s, d = u.shape
    kw = dw.shape[0]
    tm = _tile(s, 256)
    halo_per_blk = tm // HALO_ROWS
    row = pl.BlockSpec((1, d), lambda i: (0, 0))
    return pl.pallas_call(
        functools.partial(_conf_conv_kernel, kw=kw, row_chunk=min(32, tm), col_chunk=min(512, d)),
        grid=(s // tm,),
        in_specs=[
            pl.BlockSpec((tm, d), lambda i: (i, 0)),
            pl.BlockSpec((HALO_ROWS, d), lambda i: (jnp.maximum(i * halo_per_blk - 1, 0), 0)),
            pl.BlockSpec((kw, d), lambda i: (0, 0)),
            row, row, row,
        ],
        out_specs=pl.BlockSpec((tm, d), lambda i: (i, 0)),
        out_shape=jax.ShapeDtypeStruct((s, d), BF16),
        scratch_shapes=[pltpu.VMEM((HALO_ROWS + tm, d), F32), pltpu.VMEM((tm, d), F32)],
        compiler_params=_params(("parallel",)),
        name="conf_conv",
    )(u, u, dw, dw_b, ln_g, ln_b)


def _gelu_tanh(x):
    return 0.5 * x * (1.0 + jnp.tanh(0.7978845608028654 * (x + 0.044715 * (x * x * x))))


def _ffn_up_kernel(h_ref, wg_ref, wv_ref, dg_ref, dv_ref, o_ref, sg_ref, sv_ref, cg_ref, cv_ref, *, row_chunk):
    tm, tn = o_ref.shape
    i, j = pl.program_id(0), pl.program_id(1)
    hdr = SUBLANES

    @pl.when(i == 0)
    def _():
        cg_ref[j] = jnp.zeros((hdr, tn), F32)
        cv_ref[j] = jnp.zeros((hdr, tn), F32)

    sg_ref[0:hdr, :] = cg_ref[j]
    sv_ref[0:hdr, :] = cv_ref[j]
    sg_ref[hdr:hdr + tm, :] = jnp.dot(h_ref[...], wg_ref[...], preferred_element_type=F32)
    sv_ref[hdr:hdr + tm, :] = jnp.dot(h_ref[...], wv_ref[...], preferred_element_type=F32)
    cg_ref[j] = sg_ref[tm:tm + hdr, :]
    cv_ref[j] = sv_ref[tm:tm + hdr, :]

    for r0 in range(0, tm, row_chunk):
        def conv(s_ref, d_ref):
            return (s_ref[r0 + hdr - 2:r0 + hdr - 2 + row_chunk, :] * d_ref[0:1, :]
                    + s_ref[r0 + hdr - 1:r0 + hdr - 1 + row_chunk, :] * d_ref[1:2, :]
                    + s_ref[r0 + hdr:r0 + hdr + row_chunk, :] * d_ref[2:3, :])

        o_ref[r0:r0 + row_chunk, :] = (_gelu_tanh(conv(sg_ref, dg_ref)) * conv(sv_ref, dv_ref)).astype(o_ref.dtype)


def _ffn_up(h, wg, wv, dg, dv, tm=1024, tn=256):
    m, k = h.shape
    fp = wg.shape[1]
    tm, tn = _tile(m, tm), _tile(fp, tn)
    nj = fp // tn
    col = lambda i, j: (0, j)
    return pl.pallas_call(
        functools.partial(_ffn_up_kernel, row_chunk=min(128, tm)),
        grid=(m // tm, nj),
        in_specs=[
            pl.BlockSpec((tm, k), lambda i, j: (i, 0)),
            pl.BlockSpec((k, tn), col),
            pl.BlockSpec((k, tn), col),
            pl.BlockSpec((3, tn), col),
            pl.BlockSpec((3, tn), col),
        ],
        out_specs=pl.BlockSpec((tm, tn), lambda i, j: (i, j)),
        out_shape=jax.ShapeDtypeStruct((m, fp), BF16),
        scratch_shapes=[
            pltpu.VMEM((SUBLANES + tm, tn), F32), pltpu.VMEM((SUBLANES + tm, tn), F32),
            pltpu.VMEM((nj, SUBLANES, tn), F32), pltpu.VMEM((nj, SUBLANES, tn), F32),
        ],
        compiler_params=_params(("arbitrary", "arbitrary")),
        name="ffn_up",
    )(h, wg, wv, dg, dv)


def _split3(v):
    hi = v.astype(BF16)
    r1 = v - hi.astype(F32)
    mid = r1.astype(BF16)
    lo = (r1 - mid.astype(F32)).astype(BF16)
    return hi, mid, lo


def _forget_kernel(h_ref, wf_ref, bf_ref, o_ref, carry_ref):
    tm = h_ref.shape[0]

    @pl.when(pl.program_id(0) == 0)
    def _():
        carry_ref[...] = jnp.zeros_like(carry_ref)

    z = lax.dot_general(wf_ref[...], h_ref[...], (((1,), (1,)), ((), ())), preferred_element_type=F32) + bf_ref[...]
    lf = jnp.minimum(z, 0.0) - jnp.log(1.0 + jnp.exp(-jnp.abs(z)))
    upper = (lax.broadcasted_iota(jnp.int32, (tm, tm), 0) <= lax.broadcasted_iota(jnp.int32, (tm, tm), 1)).astype(BF16)
    hi, mid, lo = _split3(lf)
    cum = (jnp.dot(hi, upper, preferred_element_type=F32) + jnp.dot(mid, upper, preferred_element_type=F32)
           + jnp.dot(lo, upper, preferred_element_type=F32)) + carry_ref[:, 0:1]
    o_ref[...] = -cum
    carry_ref[...] = jnp.broadcast_to(cum[:, tm - 1:tm], carry_ref.shape)


def _forget_neg_cum(hkv, wf_t, bf_col):
    s, d = hkv.shape
    hp = wf_t.shape[0]
    tm = _tile(s, 512)
    return pl.pallas_call(
        _forget_kernel,
        grid=(s // tm,),
        in_specs=[
            pl.BlockSpec((tm, d), lambda i: (i, 0)),
            pl.BlockSpec((hp, d), lambda i: (0, 0)),
            pl.BlockSpec((hp, 1), lambda i: (0, 0)),
        ],
        out_specs=pl.BlockSpec((hp, tm), lambda i: (0, i)),
        out_shape=jax.ShapeDtypeStruct((hp, s), F32),
        scratch_shapes=[pltpu.VMEM((hp, LANES), F32)],
        compiler_params=_params(("arbitrary",)),
        name="forget_cum",
    )(hkv, wf_t, bf_col)


def _fox_kernel(q_ref, k_ref, v_ref, nck_ref, o_ref, m_ref, l_ref, acc_ref, *, tk):
    tq = q_ref.shape[0]
    i = pl.program_id(1)
    q = q_ref[...]
    m_ref[...] = jnp.full_like(m_ref, -jnp.inf)
    l_ref[...] = jnp.zeros_like(l_ref)
    acc_ref[...] = jnp.zeros_like(acc_ref)

    def step(j, masked):
        c0 = pl.multiple_of(j * tk, tk)
        kb = k_ref[pl.ds(c0, tk), :]
        s = lax.dot_general(q, kb, (((1,), (1,)), ((), ())), preferred_element_type=F32)
        s = s + nck_ref[:, pl.ds(c0, tk)]
        if masked:
            row = lax.broadcasted_iota(jnp.int32, (tq, tk), 0)
            col = lax.broadcasted_iota(jnp.int32, (tq, tk), 1)
            s = jnp.where(col <= row, s, -jnp.inf)
        m_old = m_ref[...]
        m_new = jnp.maximum(m_old, jnp.max(s, axis=-1, keepdims=True))
        alpha = jnp.exp(m_old - m_new)
        p = jnp.exp(s - m_new)
        l_ref[...] = alpha * l_ref[...] + jnp.sum(p, axis=-1, keepdims=True)
        acc_ref[...] = alpha * acc_ref[...] + jnp.dot(p.astype(BF16), v_ref[pl.ds(c0, tk), :],
                                                      preferred_element_type=F32)
        m_ref[...] = m_new

    def body(j, carry):
        step(j, False)
        return carry

    lax.fori_loop(0, i, body, 0)
    step(i, True)
    o_ref[...] = (acc_ref[...] / l_ref[...]).astype(o_ref.dtype)


def _fox_attention(q, kv, nck, n_heads, tq=512):
    s, d = q.shape
    tq = _tile(s, tq)
    return pl.pallas_call(
        functools.partial(_fox_kernel, tk=tq),
        grid=(n_heads, s // tq),
        in_specs=[
            pl.BlockSpec((tq, HEAD_DIM), lambda h, i: (i, h)),
            pl.BlockSpec((s, HEAD_DIM), lambda h, i: (0, h)),
            pl.BlockSpec((s, HEAD_DIM), lambda h, i: (0, n_heads + h)),
            pl.BlockSpec((None, 1, s), lambda h, i: (h, 0, 0)),
        ],
        out_specs=pl.BlockSpec((tq, HEAD_DIM), lambda h, i: (i, h)),
        out_shape=jax.ShapeDtypeStruct((s, d), BF16),
        scratch_shapes=[pltpu.VMEM((tq, 1), F32), pltpu.VMEM((tq, 1), F32), pltpu.VMEM((tq, HEAD_DIM), F32)],
        compiler_params=_params(("parallel", "parallel")),
        name="fox_attention",
    )(q, kv, kv, nck)


def kernel(x, c, w_cond, b_cond, w_ada, b_ada, pre_mix_g, post_mix_g, pre_ffn_g, post_ffn_g, conf_w1, conf_b1, conf_dw, conf_dw_b, conf_ln_g, conf_ln_b, conf_w2, conf_b2, w_ada_kv, b_ada_kv, kv_norm_g, w_kv, w_f, b_f, w_q, w_o, w_up, ffn_dw, w_down):
    bsz, s, d = x.shape
    depth = w_ada.shape[0]
    n_a = conf_w1.shape[0]
    n_heads = d // HEAD_DIM
    f = w_down.shape[1]
    fp = _round_up(f, 2 * LANES)
    scale = HEAD_DIM ** -0.5

    w1_b, w2_b = conf_w1.astype(BF16), conf_w2.astype(BF16)
    wkv_b, wq_b, wo_b = w_kv.astype(BF16), w_q.astype(BF16), w_o.astype(BF16)
    pad_c = ((0, 0), (0, 0), (0, fp - f))
    wg_b = jnp.pad(w_up[:, :, :f].astype(BF16), pad_c)
    wv_b = jnp.pad(w_up[:, :, f:].astype(BF16), pad_c)
    dg = jnp.pad(ffn_dw[:, :, :f], pad_c)
    dv = jnp.pad(ffn_dw[:, :, f:], pad_c)
    wd_b = jnp.pad(w_down.astype(BF16), ((0, 0), (0, fp - f), (0, 0)))
    hp = _round_up(n_heads, SUBLANES)
    wf_t = jnp.pad(w_f.T.astype(BF16), ((0, hp - n_heads), (0, 0)))
    bf_col = jnp.pad(b_f, (0, hp - n_heads)).reshape(hp, 1)

    def row(v):
        return v.reshape(1, -1)

    outs = []
    for b in range(bsz):
        xb = x[b]
        cond = _cond(c[b].reshape(d, 1), w_cond, b_cond)
        cond_col = cond.reshape(-1, 1)
        mod = _mod(cond_col, w_ada, b_ada)
        mod_kv = _mod(cond_col, w_ada_kv[None], b_ada_kv[None])[0]

        def mods(l):
            return [mod[l, :, t * d:(t + 1) * d] for t in range(6)]

        sh_m, sc_m, g_m, sh_f, sc_f, g_f = mods(0)
        h = _prenorm(xb, row(pre_mix_g[0]), sc_m, sh_m)
        kv = nck = None
        for l in range(depth):
            sh_m, sc_m, g_m, sh_f, sc_f, g_f = mods(l)
            if l < n_a:
                u = _glu_matmul(h, w1_b[l], row(conf_b1[l]))
                v = _conf_conv(u, conf_dw[l], row(conf_dw_b[l]), row(conf_ln_g[l]), row(conf_ln_b[l]))
                y = _matmul(v, w2_b[l], bias=row(conf_b2[l]))
            else:
                q = _matmul(h, wq_b[l - n_a], scale=scale, out_dtype=BF16)
                o = _fox_attention(q, kv, nck, n_heads)
                y = _matmul(o, wo_b[l - n_a])
            xb, (h,) = _post(y, xb, row(post_mix_g[l]), g_m, [(row(pre_ffn_g[l]), sc_f, sh_f)])
            h2 = _ffn_up(h, wg_b[l], wv_b[l], dg[l], dv[l])
            y = _matmul(h2, wd_b[l], tm=512, tn=512)
            if l + 1 == depth:
                xb, _ = _post(y, xb, row(post_ffn_g[l]), g_f, [])
            else:
                nsh, nsc = mods(l + 1)[0], mods(l + 1)[1]
                norms = [(row(pre_mix_g[l + 1]), nsc, nsh)]
                if l + 1 == n_a:
                    norms.append((row(kv_norm_g), mod_kv[:, d:], mod_kv[:, :d]))
                xb, hs = _post(y, xb, row(post_ffn_g[l]), g_f, norms)
                h = hs[0]
                if l + 1 == n_a:
                    hkv = hs[1]
                    kv = _matmul(hkv, wkv_b, out_dtype=BF16)
                    nck = _forget_neg_cum(hkv, wf_t, bf_col)[:n_heads].reshape(n_heads, 1, s)
        outs.append(xb)
    return jnp.stack(outs)
```
